```python
import jax
import jax.numpy as jnp
from jax import lax
import numpy as np

D_MODEL = 1024
BATCH = 16
SEQ = 2048
DEPTH = 2
DEC_BATCH = 4
DEC_SEQ = 4096
PAST_LEN = 128

GRID_W = 64
EPS = 1e-6

SSD_HEADS = 16
SSD_HEAD_DIM = 64
SSD_WIDTH = SSD_HEADS * SSD_HEAD_DIM
SSD_GROUPS = 4
SSD_STATE = 128
SSD_CONV = 5
SSD_CHUNK = 128
SSD_CONV_CH = SSD_WIDTH + 2 * SSD_GROUPS * SSD_STATE
DT_MIN = 0.001
DT_MAX = 0.1

GLA_HEADS = 4
GLA_DK = 64
GLA_DV = 128
GLA_QK_WIDTH = GLA_HEADS * GLA_DK
GLA_WIDTH = GLA_HEADS * GLA_DV
GLA_RANK = 16
GLA_TAU = 16.0
GLA_CHUNK = 64

ATT_HEADS = 8
ATT_KV_HEADS = 2
ATT_HEAD_DIM = 64
ATT_WIDTH = ATT_HEADS * ATT_HEAD_DIM
ATT_KV_WIDTH = ATT_KV_HEADS * ATT_HEAD_DIM
ROPE_THETA = 10000.0
ROPE_N_FREQ = ATT_HEAD_DIM // 4
Q_BLOCK = 128

N_BRANCH = 3
D_FF = -(-8 * D_MODEL // (3 * 256)) * 256

IN_SIZES = (SSD_WIDTH, SSD_CONV_CH, 2 * SSD_HEADS,
            GLA_QK_WIDTH, GLA_QK_WIDTH, GLA_WIDTH, GLA_WIDTH, 2 * GLA_RANK,
            ATT_WIDTH, ATT_KV_WIDTH, ATT_KV_WIDTH,
            N_BRANCH * D_MODEL)
IN_WIDTH = (SSD_WIDTH + SSD_CONV_CH + 2 * SSD_HEADS
            + 2 * GLA_QK_WIDTH + 2 * GLA_WIDTH + 2 * GLA_RANK
            + ATT_WIDTH + 2 * ATT_KV_WIDTH + N_BRANCH * D_MODEL)

kernel_name = 'hybrid_ssd_gla_axialgqa_encoder'


def _rmsnorm(x, w):
    xf = x.astype(jnp.float32)
    y = xf * lax.rsqrt(jnp.mean(xf * xf, axis=-1, keepdims=True) + EPS)
    return (y * w.astype(jnp.float32)).astype(x.dtype)


def _split(t, sizes):
    cuts = []
    acc = 0
    for s in sizes[:-1]:
        acc += s
        cuts.append(acc)
    return jnp.split(t, cuts, axis=-1)


def _rev(t):
    return jnp.flip(t, axis=1)


def _centred_depthwise_conv(x, w, b):
    y = lax.conv_general_dilated(
        x, w[:, None, :].astype(x.dtype), window_strides=(1,),
        padding=[(SSD_CONV // 2, SSD_CONV // 2)],
        dimension_numbers=('NWC', 'WIO', 'NWC'),
        feature_group_count=x.shape[-1])
    return y + b.astype(x.dtype)


def _segsum_exp(a):
    cs = jnp.cumsum(a, axis=-1)
    q = a.shape[-1]
    mask = jnp.tril(jnp.ones((q, q), dtype=bool))
    return jnp.exp(jnp.where(mask, cs[..., :, None] - cs[..., None, :], -jnp.inf))


def _ssd_chunked_scan(x, dt, a, b_mat, c_mat):
    bsz, length, n_heads, p = x.shape
    g, n = b_mat.shape[-2:]
    hg = n_heads // g
    nc = length // SSD_CHUNK
    xdt = (x * dt[..., None]).reshape(bsz, nc, SSD_CHUNK, g, hg, p)
    adt = jnp.moveaxis((dt * a).reshape(bsz, nc, SSD_CHUNK, g, hg), 2, -1)
    bc = b_mat.reshape(bsz, nc, SSD_CHUNK, g, n)
    cc = c_mat.reshape(bsz, nc, SSD_CHUNK, g, n)
    a_cs = jnp.cumsum(adt, axis=-1)
    decay_mat = _segsum_exp(adt)
    scores = jnp.einsum('bclgn,bcsgn->bcgls', cc, bc)
    w = scores[:, :, :, None] * decay_mat
    y_diag = jnp.einsum('bcghls,bcsghp->bclghp', w, xdt)
    decay_to_end = jnp.moveaxis(jnp.exp(a_cs[..., -1:] - a_cs), -1, 2)
    states = jnp.einsum('bclgn,bclghp->bcghpn', bc, xdt * decay_to_end[..., None])
    chunk_decay = jnp.exp(a_cs[..., -1])

    def step(s, inp):
        st, dec = inp
        return s * dec[..., None, None] + st, s

    init = jnp.zeros((bsz, g, hg, p, n), jnp.float32)
    _, prev = lax.scan(step, init, (jnp.moveaxis(states, 1, 0), jnp.moveaxis(chunk_decay, 1, 0)))
    prev = jnp.moveaxis(prev, 0, 1)
    y_off = jnp.einsum('bclgn,bcghpn->bclghp', cc, prev) * jnp.moveaxis(jnp.exp(a_cs), -1, 2)[..., None]
    return (y_diag + y_off).reshape(bsz, length, n_heads, p)


def _ssd_branch(z, xbc, dt_raw, conv_w, conv_b, a_log_f, a_log_b, dt_bias_f, dt_bias_b, d_skip, norm_w):
    f32 = jnp.float32
    bsz, length = z.shape[:2]
    xbc = jax.nn.silu(_centred_depthwise_conv(xbc, conv_w, conv_b)).astype(f32)
    xs, bm, cm = _split(xbc, (SSD_WIDTH, SSD_GROUPS * SSD_STATE, SSD_GROUPS * SSD_STATE))
    xs = xs.reshape(bsz, length, SSD_HEADS, SSD_HEAD_DIM)
    bm = bm.reshape(bsz, length, SSD_GROUPS, SSD_STATE)
    cm = cm.reshape(bsz, length, SSD_GROUPS, SSD_STATE)
    dt_raw = dt_raw.astype(f32)
    dt_f = jax.nn.softplus(dt_raw[..., :SSD_HEADS] + dt_bias_f.astype(f32))
    dt_b = jax.nn.softplus(dt_raw[..., SSD_HEADS:] + dt_bias_b.astype(f32))
    a_f = -jnp.exp(a_log_f.astype(f32))
    a_b = -jnp.exp(a_log_b.astype(f32))
    y_f = _ssd_chunked_scan(xs, dt_f, a_f, bm, cm)
    y_b = _rev(_ssd_chunked_scan(_rev(xs), _rev(dt_b), a_b, _rev(bm), _rev(cm)))
    y = y_f + y_b + xs * d_skip.astype(f32)[:, None]
    y = y.reshape(bsz, length, SSD_WIDTH) * jax.nn.silu(z.astype(f32))
    return _rmsnorm(y, norm_w).astype(z.dtype)


def _gla_chunked_scan(q, k, v, g):
    bsz, length, n_heads, dk = q.shape
    dv = v.shape[-1]
    nc = length // GLA_CHUNK
    q = q.reshape(bsz, nc, GLA_CHUNK, n_heads, dk)
    k = k.reshape(bsz, nc, GLA_CHUNK, n_heads, dk)
    v = v.reshape(bsz, nc, GLA_CHUNK, n_heads, dv)
    g = g.reshape(bsz, nc, GLA_CHUNK, n_heads, dk)
    bcum = jnp.cumsum(g, axis=2)
    ref = bcum[:, :, GLA_CHUNK // 2:GLA_CHUNK // 2 + 1]
    att = jnp.einsum('bcihk,bcjhk->bchij', q * jnp.exp(bcum - ref), k * jnp.exp(ref - bcum))
    mask = jnp.tril(jnp.ones((GLA_CHUNK, GLA_CHUNK), dtype=bool))
    o_intra = jnp.einsum('bchij,bcjhv->bcihv', jnp.where(mask, att, 0.0), v)
    last = bcum[:, :, -1:]
    s_chunk = jnp.einsum('bcjhk,bcjhv->bchkv', k * jnp.exp(last - bcum), v)
    chunk_decay = jnp.exp(last[:, :, 0])

    def step(s, inp):
        st, dec = inp
        return s * dec[..., None] + st, s

    init = jnp.zeros((bsz, n_heads, dk, dv), jnp.float32)
    _, prev = lax.scan(step, init, (jnp.moveaxis(s_chunk, 1, 0), jnp.moveaxis(chunk_decay, 1, 0)))
    prev = jnp.moveaxis(prev, 0, 1)
    o_inter = jnp.einsum('bcihk,bchkv->bcihv', q * jnp.exp(bcum), prev)
    return (o_intra + o_inter).reshape(bsz, length, n_heads, dv)


def _gla_branch(q, k, v, out_gate, lowrank, w2_f, b_f, w2_b, b_b, norm_w):
    f32 = jnp.float32
    bsz, length = q.shape[:2]
    q = q.astype(f32).reshape(bsz, length, GLA_HEADS, GLA_DK) * (GLA_DK ** -0.5)
    k = k.astype(f32).reshape(bsz, length, GLA_HEADS, GLA_DK)
    v = v.astype(f32).reshape(bsz, length, GLA_HEADS, GLA_DV)
    lowrank = lowrank.astype(f32)
    g_f = jax.nn.log_sigmoid(lowrank[..., :GLA_RANK] @ w2_f.astype(f32) + b_f.astype(f32)) / GLA_TAU
    g_b = jax.nn.log_sigmoid(lowrank[..., GLA_RANK:] @ w2_b.astype(f32) + b_b.astype(f32)) / GLA_TAU
    g_f = g_f.reshape(bsz, length, GLA_HEADS, GLA_DK)
    g_b = g_b.reshape(bsz, length, GLA_HEADS, GLA_DK)
    o = _gla_chunked_scan(q, k, v, g_f) + _rev(_gla_chunked_scan(_rev(q), _rev(k), _rev(v), _rev(g_b)))
    o = _rmsnorm(o, norm_w).reshape(bsz, length, GLA_WIDTH)
    return (o * jax.nn.silu(out_gate.astype(f32))).astype(out_gate.dtype)


def _axial_rope_tables(length):
    rows = length // GRID_W
    row = jnp.broadcast_to(jnp.arange(rows, dtype=jnp.float32)[:, None], (rows, GRID_W)).reshape(length)
    col = jnp.broadcast_to(jnp.arange(GRID_W, dtype=jnp.float32)[None, :], (rows, GRID_W)).reshape(length)
    inv = jnp.float32(ROPE_THETA) ** (-jnp.arange(ROPE_N_FREQ, dtype=jnp.float32) / ROPE_N_FREQ)
    ang = jnp.stack([row[:, None] * inv, col[:, None] * inv], axis=1)
    return jnp.cos(ang), jnp.sin(ang)


def _apply_axial_rope(x, cos, sin):
    bsz, length, n_heads, d = x.shape
    xr = x.astype(jnp.float32).reshape(bsz, length, n_heads, 2, 2, ROPE_N_FREQ)
    x1 = xr[..., 0, :]
    x2 = xr[..., 1, :]
    c = cos[None, :, None]
    s = sin[None, :, None]
    out = jnp.stack([x1 * c - x2 * s, x2 * c + x1 * s], axis=-2)
    return out.reshape(bsz, length, n_heads, d)


def _attn_branch(q, k, v, q_norm_w, k_norm_w):
    f32 = jnp.float32
    bsz, length = q.shape[:2]
    grp = ATT_HEADS // ATT_KV_HEADS
    q = _rmsnorm(q.astype(f32).reshape(bsz, length, ATT_HEADS, ATT_HEAD_DIM), q_norm_w)
    k = _rmsnorm(k.astype(f32).reshape(bsz, length, ATT_KV_HEADS, ATT_HEAD_DIM), k_norm_w)
    v_h = v.astype(f32).reshape(bsz, length, ATT_KV_HEADS, ATT_HEAD_DIM)
    cos, sin = _axial_rope_tables(length)
    q = _apply_axial_rope(q, cos, sin) * (ATT_HEAD_DIM ** -0.5)
    k = _apply_axial_rope(k, cos, sin)
    qb = q.reshape(bsz, length // Q_BLOCK, Q_BLOCK, ATT_KV_HEADS, grp, ATT_HEAD_DIM)
    qb = jnp.moveaxis(qb, 1, 0)

    def block(qi):
        s = jnp.einsum('bqkgd,bskd->bkgqs', qi, k)
        p = jax.nn.softmax(s, axis=-1)
        return jnp.einsum('bkgqs,bskd->bqkgd', p, v_h)

    o = lax.map(block, qb)
    o = jnp.moveaxis(o, 0, 1).reshape(bsz, length, ATT_WIDTH)
    return o.astype(v.dtype)


def _layer(x, w_in, conv_w, conv_b, ssd_a_log_f, ssd_a_log_b, ssd_dt_bias_f, ssd_dt_bias_b, ssd_d,
           ssd_norm_w, gla_w2_f, gla_b_f, gla_w2_b, gla_b_b, gla_norm_w, att_q_norm_w, att_k_norm_w,
           w_br_ssd, w_br_gla, w_br_attn, w_out, norm1_w, norm2_w, w_ffn_in, w_ffn_out):
    bsz, length, _ = x.shape
    h = _rmsnorm(x, norm1_w)
    proj = h @ w_in
    (z, xbc, dt_raw, gq, gk, gv, gog, glr, aq, ak, av, gates) = _split(proj, IN_SIZES)
    o_ssd = _ssd_branch(z, xbc, dt_raw, conv_w, conv_b, ssd_a_log_f, ssd_a_log_b,
                        ssd_dt_bias_f, ssd_dt_bias_b, ssd_d, ssd_norm_w)
    o_gla = _gla_branch(gq, gk, gv, gog, glr, gla_w2_f, gla_b_f, gla_w2_b, gla_b_b, gla_norm_w)
    o_att = _attn_branch(aq, ak, av, att_q_norm_w, att_k_norm_w)
    g = jax.nn.sigmoid(gates.astype(jnp.float32)).reshape(bsz, length, N_BRANCH, D_MODEL).astype(x.dtype)
    merged = (g[:, :, 0] * (o_ssd @ w_br_ssd)
              + g[:, :, 1] * (o_gla @ w_br_gla)
              + g[:, :, 2] * (o_att @ w_br_attn))
    x = x + merged @ w_out
    h2 = _rmsnorm(x, norm2_w)
    gate, up = _split(h2 @ w_ffn_in, (D_FF, D_FF))
    return x + (jax.nn.silu(gate) * up) @ w_ffn_out


def _trunk(x, w_in, conv_w, conv_b, ssd_a_log_f, ssd_a_log_b, ssd_dt_bias_f, ssd_dt_bias_b, ssd_d,
           ssd_norm_w, gla_w2_f, gla_b_f, gla_w2_b, gla_b_b, gla_norm_w, att_q_norm_w, att_k_norm_w,
           w_br_ssd, w_br_gla, w_br_attn, w_out, norm1_w, norm2_w, w_ffn_in, w_ffn_out, final_norm_w):
    for l in range(DEPTH):
        x = _layer(x, w_in[l], conv_w[l], conv_b[l], ssd_a_log_f[l], ssd_a_log_b[l], ssd_dt_bias_f[l],
                   ssd_dt_bias_b[l], ssd_d[l], ssd_norm_w[l], gla_w2_f[l], gla_b_f[l], gla_w2_b[l],
                   gla_b_b[l], gla_norm_w[l], att_q_norm_w[l], att_k_norm_w[l], w_br_ssd[l],
                   w_br_gla[l], w_br_attn[l], w_out[l], norm1_w[l], norm2_w[l], w_ffn_in[l], w_ffn_out[l])
    return _rmsnorm(x, final_norm_w)


def setup_inputs(seed: int = 0) -> dict:
    key = jax.random.key(seed)
    ks = jax.random.split(key, 32)
    f32 = jnp.float32

    def nrm(k, shape, fan_in):
        return jax.random.normal(k, shape, f32) * (fan_in ** -0.5)

    def gain(k, shape):
        return 1.0 + 0.02 * jax.random.normal(k, shape, f32)

    def dt_bias(k):
        dt0 = jnp.exp(jax.random.uniform(k, (DEPTH, SSD_HEADS), f32, jnp.log(DT_MIN), jnp.log(DT_MAX)))
        return dt0 + jnp.log(-jnp.expm1(-dt0))

    def a_log(k):
        return jnp.log(jax.random.uniform(k, (DEPTH, SSD_HEADS), f32, 1.0, 16.0))

    return {
        'x_prompt': jax.random.normal(ks[0], (BATCH, SEQ, D_MODEL), f32),
        'x_sample': jax.random.normal(ks[1], (DEC_BATCH, DEC_SEQ, D_MODEL), f32),
        'w_in': nrm(ks[2], (DEPTH, D_MODEL, IN_WIDTH), D_MODEL),
        'conv_w': nrm(ks[3], (DEPTH, SSD_CONV, SSD_CONV_CH), SSD_CONV),
        'conv_b': 0.01 * jax.random.normal(ks[4], (DEPTH, SSD_CONV_CH), f32),
        'ssd_a_log_f': a_log(ks[5]),
        'ssd_a_log_b': a_log(ks[6]),
        'ssd_dt_bias_f': dt_bias(ks[7]),
        'ssd_dt_bias_b': dt_bias(ks[8]),
        'ssd_d': 1.0 + 0.1 * jax.random.normal(ks[9], (DEPTH, SSD_HEADS), f32),
        'ssd_norm_w': gain(ks[10], (DEPTH, SSD_WIDTH)),
        'gla_w2_f': nrm(ks[11], (DEPTH, GLA_RANK, GLA_QK_WIDTH), GLA_RANK),
        'gla_b_f': 0.1 * jax.random.normal(ks[12], (DEPTH, GLA_QK_WIDTH), f32),
        'gla_w2_b': nrm(ks[13], (DEPTH, GLA_RANK, GLA_QK_WIDTH), GLA_RANK),
        'gla_b_b': 0.1 * jax.random.normal(ks[14], (DEPTH, GLA_QK_WIDTH), f32),
        'gla_norm_w': gain(ks[15], (DEPTH, GLA_DV)),
        'att_q_norm_w': gain(ks[16], (DEPTH, ATT_HEAD_DIM)),
        'att_k_norm_w': gain(ks[17], (DEPTH, ATT_HEAD_DIM)),
        'w_br_ssd': nrm(ks[18], (DEPTH, SSD_WIDTH, D_MODEL), SSD_WIDTH),
        'w_br_gla': nrm(ks[19], (DEPTH, GLA_WIDTH, D_MODEL), GLA_WIDTH),
        'w_br_attn': nrm(ks[20], (DEPTH, ATT_WIDTH, D_MODEL), ATT_WIDTH),
        'w_out': nrm(ks[21], (DEPTH, D_MODEL, D_MODEL), D_MODEL),
        'norm1_w': gain(ks[22], (DEPTH, D_MODEL)),
        'norm2_w': gain(ks[23], (DEPTH, D_MODEL)),
        'w_ffn_in': nrm(ks[24], (DEPTH, D_MODEL, 2 * D_FF), D_MODEL),
        'w_ffn_out': nrm(ks[25], (DEPTH, D_FF, D_MODEL), D_FF),
        'final_norm_w': gain(ks[26], (D_MODEL,)),
    }


def reference(x_prompt, x_sample, w_in, conv_w, conv_b, ssd_a_log_f, ssd_a_log_b, ssd_dt_bias_f,
              ssd_dt_bias_b, ssd_d, ssd_norm_w, gla_w2_f, gla_b_f, gla_w2_b, gla_b_b, gla_norm_w,
              att_q_norm_w, att_k_norm_w, w_br_ssd, w_br_gla, w_br_attn, w_out, norm1_w, norm2_w,
              w_ffn_in, w_ffn_out, final_norm_w):
    params = (w_in, conv_w, conv_b, ssd_a_log_f, ssd_a_log_b, ssd_dt_bias_f, ssd_dt_bias_b, ssd_d,
              ssd_norm_w, gla_w2_f, gla_b_f, gla_w2_b, gla_b_b, gla_norm_w, att_q_norm_w, att_k_norm_w,
              w_br_ssd, w_br_gla, w_br_attn, w_out, norm1_w, norm2_w, w_ffn_in, w_ffn_out, final_norm_w)
    y_prompt = _trunk(x_prompt, *params)
    y_sample = _trunk(x_sample, *params)
    return (y_prompt, y_sample)
```

```python
import functools

import jax
import jax.numpy as jnp
from jax import lax
from jax.experimental import pallas as pl
from jax.experimental.pallas import tpu as pltpu

F32 = jnp.float32
BF16 = jnp.bfloat16

D_MODEL = 1024
EPS = 1e-6
GRID_W = 64

SSD_HEADS = 16
SSD_HEAD_DIM = 64
SSD_WIDTH = 1024
SSD_GROUPS = 4
SSD_STATE = 128
SSD_CONV = 5
SSD_CHUNK = 128
SSD_CONV_CH = 2048

GLA_HEADS = 4
GLA_DK = 64
GLA_DV = 128
GLA_QK_WIDTH = 256
GLA_WIDTH = 512
GLA_RANK = 16
GLA_TAU = 16.0
GLA_CHUNK = 64

ATT_HEADS = 8
ATT_KV_HEADS = 2
ATT_HEAD_DIM = 64
ATT_WIDTH = 512
ATT_KV_WIDTH = 128
ROPE_THETA = 10000.0
ROPE_N_FREQ = 16

D_FF = 2816

LANES = 128
HALF = 64
VMEM_LIMIT = 56 * 1024 * 1024

PROJ_W = 8704
C_GATES = 0
C_Z = 3072
C_XBC = 4096
C_GV = 6144
C_GOG = 6656
C_AQ = 7168
C_GQ = 7680
C_GK = 7936
C_AK = 8192
C_AV = 8320
C_DT = 8448
C_GLR = 8576


def _params(sem, limit=VMEM_LIMIT):
    return pltpu.CompilerParams(dimension_semantics=sem, vmem_limit_bytes=limit)


def _split3(a):
    hi = a.astype(BF16)
    r = a - hi.astype(F32)
    mid = r.astype(BF16)
    lo = (r - mid.astype(F32)).astype(BF16)
    return hi, mid, lo


def _exact_ldot(m, a):
    hi, mid, lo = _split3(a)
    d = lambda u: jnp.dot(m, u, preferred_element_type=F32)
    return d(hi) + d(mid) + d(lo)


def _exact_rdot(a, m):
    hi, mid, lo = _split3(a)
    d = lambda u: jnp.dot(u, m, preferred_element_type=F32)
    return d(hi) + d(mid) + d(lo)


def _sigmoid(x):
    return 1.0 / (1.0 + jnp.exp(-x))


def _silu(x):
    return x * _sigmoid(x)


def _softplus(x):
    return jnp.maximum(x, 0.0) + jnp.log(1.0 + jnp.exp(-jnp.abs(x)))


def _dot_nt(a, b):
    return lax.dot_general(a, b, (((1,), (1,)), ((), ())), preferred_element_type=F32)


def _in_proj_kernel(x_ref, g_ref, w_ref, o_ref, h_ref):
    @pl.when(pl.program_id(1) == 0)
    def _():
        x = x_ref[...]
        ms = jnp.mean(x * x, axis=-1, keepdims=True)
        h_ref[...] = (x * lax.rsqrt(ms + EPS) * g_ref[...]).astype(BF16)

    o_ref[...] = jnp.dot(h_ref[...], w_ref[...], preferred_element_type=F32).astype(o_ref.dtype)


def _in_proj(x2d, g, w, tm, tn):
    t, d = x2d.shape
    n = w.shape[1]
    return pl.pallas_call(
        _in_proj_kernel,
        grid=(t // tm, n // tn),
        in_specs=[pl.BlockSpec((tm, d), lambda i, j: (i, 0)),
                  pl.BlockSpec((1, d), lambda i, j: (0, 0)),
                  pl.BlockSpec((d, tn), lambda i, j: (0, j))],
        out_specs=pl.BlockSpec((tm, tn), lambda i, j: (i, j)),
        out_shape=jax.ShapeDtypeStruct((t, n), BF16),
        scratch_shapes=[pltpu.VMEM((tm, d), BF16)],
        compiler_params=_params(("parallel", "arbitrary")),
        name="in_proj",
    )(x2d, g, w)


CONV_HALO = 16
CONV_CW = 256


def _conv_kernel(xm_ref, xp_ref, xn_ref, w_ref, b_ref, o_ref, ext_ref, *, tl):
    i = pl.program_id(1)
    first = i == 0
    last = i == pl.num_programs(1) - 1
    pad = SSD_CONV // 2
    ext_ref[0:CONV_HALO, :] = jnp.where(first, 0.0, xp_ref[...].astype(F32))
    ext_ref[CONV_HALO:CONV_HALO + tl, :] = xm_ref[...].astype(F32)
    ext_ref[CONV_HALO + tl:, :] = jnp.where(last, 0.0, xn_ref[...].astype(F32))
    for c0 in range(0, SSD_CONV_CH, CONV_CW):
        acc = jnp.zeros((tl, CONV_CW), F32) + b_ref[:, c0:c0 + CONV_CW]
        for k in range(SSD_CONV):
            s0 = CONV_HALO - pad + k
            acc = acc + ext_ref[s0:s0 + tl, c0:c0 + CONV_CW] * w_ref[k:k + 1, c0:c0 + CONV_CW]
        o_ref[:, c0:c0 + CONV_CW] = _silu(acc).astype(o_ref.dtype)


def _conv(proj, conv_w, conv_b, tl):
    b, l, _ = proj.shape
    nh = tl // CONV_HALO
    cblk = C_XBC // SSD_CONV_CH
    return pl.pallas_call(
        functools.partial(_conv_kernel, tl=tl),
        grid=(b, l // tl),
        in_specs=[
            pl.BlockSpec((None, tl, SSD_CONV_CH), lambda bi, i: (bi, i, cblk)),
            pl.BlockSpec((None, CONV_HALO, SSD_CONV_CH),
                         lambda bi, i: (bi, jnp.maximum(i * nh - 1, 0), cblk)),
            pl.BlockSpec((None, CONV_HALO, SSD_CONV_CH),
                         lambda bi, i: (bi, jnp.minimum((i + 1) * nh, l // CONV_HALO - 1), cblk)),
            pl.BlockSpec((8, SSD_CONV_CH), lambda bi, i: (0, 0)),
            pl.BlockSpec((1, SSD_CONV_CH), lambda bi, i: (0, 0)),
        ],
        out_specs=pl.BlockSpec((None, tl, SSD_CONV_CH), lambda bi, i: (bi, i, 0)),
        out_shape=jax.ShapeDtypeStruct((b, l, SSD_CONV_CH), BF16),
        scratch_shapes=[pltpu.VMEM((tl + 2 * CONV_HALO, SSD_CONV_CH), F32)],
        compiler_params=_params(("parallel", "parallel")),
        name="ssd_conv",
    )(proj, proj, proj, conv_w, conv_b)


def _ssd_direction(xs_ref, b_ref, c_ref, dt_ref, h_ref, y_ref, bias_row, alog_row, dskip_ref,
                   *, backward):
    q = SSD_CHUNK
    lane_off = SSD_HEADS if backward else 0
    last = 0 if backward else q - 1

    row = lax.broadcasted_iota(jnp.int32, (q, q), 0)
    col = lax.broadcasted_iota(jnp.int32, (q, q), 1)
    keep = (col >= row) if backward else (row >= col)
    tri = jnp.where(keep, 1.0, 0.0).astype(BF16)
    lo_f = jnp.where(lax.broadcasted_iota(jnp.int32, (1, LANES), 1) < HALF, 1.0, 0.0)
    hi_f = 1.0 - lo_f
    lo_b = lo_f.astype(BF16)
    hi_b = hi_f.astype(BF16)

    dt = _softplus(dt_ref[...].astype(F32) + bias_row)
    adt = dt * (-jnp.exp(alog_row))
    a_cs = _exact_ldot(tri, adt)
    a_cs_t = a_cs.T
    dt_t = dt.T

    for g in range(SSD_GROUPS):
        bg = b_ref[:, g * SSD_STATE:(g + 1) * SSD_STATE]
        cg = c_ref[:, g * SSD_STATE:(g + 1) * SSD_STATE]
        scores = _dot_nt(cg, bg)
        bg_t = bg.astype(F32).T
        cg_f = cg.astype(F32)
        for jp in range(2):
            pair = 2 * g + jp
            w_l, coff_l, bv_l, dec_l = [], [], [], []
            for hh in range(2):
                lane = lane_off + 2 * pair + hh
                colb = jnp.broadcast_to(a_cs[:, lane:lane + 1], (q, q))
                rowb = a_cs_t[lane:lane + 1, :]
                dtrow = dt_t[lane:lane + 1, :]
                decay = jnp.where(keep, jnp.exp(colb - rowb), 0.0)
                w_l.append(scores * decay * dtrow)
                coff_l.append(cg_f * jnp.exp(colb))
                lastb = colb[last:last + 1, :]
                bv_l.append(bg_t * (dtrow * jnp.exp(lastb - rowb)))
                dec_l.append(jnp.exp(lastb))
            sl = slice(pair * LANES, (pair + 1) * LANES)
            xs_p = xs_ref[:, sl]
            h_p = h_ref[:, sl]
            h_b = h_p.astype(BF16)
            xs_lo = xs_p * lo_b
            xs_hi = xs_p * hi_b
            lhs_y = jnp.concatenate(w_l + coff_l, axis=1).astype(BF16)
            rhs_y = jnp.concatenate([xs_lo, xs_hi, h_b * lo_b, h_b * hi_b], axis=0)
            y = jnp.dot(lhs_y, rhs_y, preferred_element_type=F32)
            if dskip_ref is not None:
                y = y + xs_p.astype(F32) * dskip_ref[:, sl]
            y_ref[:, sl] = y.astype(y_ref.dtype)
            lhs_s = jnp.concatenate(bv_l, axis=1).astype(BF16)
            rhs_s = jnp.concatenate([xs_lo, xs_hi], axis=0)
            s_new = jnp.dot(lhs_s, rhs_s, preferred_element_type=F32)
            dec = dec_l[0] * lo_f + dec_l[1] * hi_f
            h_ref[:, sl] = h_p * dec + s_new


def _ssd_kernel(xs_f, b_f, c_f, dt_f, xs_b, b_b, c_b, dt_b, bias_ref, alog_ref, dskip_ref,
                yf_ref, yb_ref, hf_ref, hb_ref):
    @pl.when(pl.program_id(1) == 0)
    def _():
        hf_ref[...] = jnp.zeros_like(hf_ref)
        hb_ref[...] = jnp.zeros_like(hb_ref)

    bias_row = bias_ref[...]
    alog_row = alog_ref[...]
    _ssd_direction(xs_f, b_f, c_f, dt_f, hf_ref, yf_ref, bias_row, alog_row, dskip_ref,
                   backward=False)
    _ssd_direction(xs_b, b_b, c_b, dt_b, hb_ref, yb_ref, bias_row, alog_row, None,
                   backward=True)


def _ssd(xbc, proj, bias_row, alog_row, dskip_row):
    b, l, _ = xbc.shape
    q = SSD_CHUNK
    nc = l // q
    bc_w = SSD_GROUPS * SSD_STATE
    fwd = lambda bi, c: c
    bwd = lambda bi, c: nc - 1 - c

    def specs(ci):
        return [
            pl.BlockSpec((None, q, SSD_WIDTH), lambda bi, c: (bi, ci(bi, c), 0)),
            pl.BlockSpec((None, q, bc_w), lambda bi, c: (bi, ci(bi, c), SSD_WIDTH // bc_w)),
            pl.BlockSpec((None, q, bc_w), lambda bi, c: (bi, ci(bi, c), SSD_WIDTH // bc_w + 1)),
            pl.BlockSpec((None, q, LANES), lambda bi, c: (bi, ci(bi, c), C_DT // LANES)),
        ]

    row = lambda n: pl.BlockSpec((1, n), lambda bi, c: (0, 0))
    return pl.pallas_call(
        _ssd_kernel,
        grid=(b, nc),
        in_specs=specs(fwd) + specs(bwd) + [row(LANES), row(LANES), row(SSD_WIDTH)],
        out_specs=[pl.BlockSpec((None, q, SSD_WIDTH), lambda bi, c: (bi, c, 0)),
                   pl.BlockSpec((None, q, SSD_WIDTH), lambda bi, c: (bi, nc - 1 - c, 0))],
        out_shape=[jax.ShapeDtypeStruct((b, l, SSD_WIDTH), BF16)] * 2,
        scratch_shapes=[pltpu.VMEM((SSD_STATE, SSD_WIDTH), F32)] * 2,
        compiler_params=_params(("parallel", "arbitrary")),
        name="ssd_scan",
    )(xbc, xbc, xbc, proj, xbc, xbc, xbc, proj, bias_row, alog_row, dskip_row)


GLA_TILE = 2 * GLA_CHUNK


def _gla_direction(q_ref, k_ref, v_ref, lr_ref, w2_ref, bias_ref, s_ref, o_ref, *, backward):
    t = GLA_TILE
    c = GLA_CHUNK
    row = lax.broadcasted_iota(jnp.int32, (t, t), 0)
    col = lax.broadcasted_iota(jnp.int32, (t, t), 1)
    same = (row < c) == (col < c)
    order = (col >= row) if backward else (row >= col)
    keep = same & order
    tri = jnp.where(keep, 1.0, 0.0).astype(BF16)
    first_rows = (c, t) if backward else (0, c)
    ref_a, ref_b = (c // 2 - 1, c + c // 2 - 1) if backward else (c // 2, c + c // 2)
    last_a, last_b = (0, c) if backward else (c - 1, t - 1)

    rsel = lax.broadcasted_iota(jnp.int32, (t, 1), 0) < c
    row_a = jnp.where(rsel, 1.0, 0.0)
    row_b = 1.0 - row_a
    lsel = lax.broadcasted_iota(jnp.int32, (1, LANES), 1) < HALF
    lo_f = jnp.where(lsel, 1.0, 0.0)
    hi_f = 1.0 - lo_f
    col_a = lo_f
    col_b = hi_f

    lr = lr_ref[...]
    x = jnp.dot(lr, w2_ref[...], preferred_element_type=F32) + bias_ref[...]
    g = (jnp.minimum(x, 0.0) - jnp.log(1.0 + jnp.exp(-jnp.abs(x)))) * (1.0 / GLA_TAU)
    bcum = _exact_ldot(tri, g)
    refb = jnp.where(rsel, bcum[ref_a:ref_a + 1, :], bcum[ref_b:ref_b + 1, :])
    lastb = jnp.where(rsel, bcum[last_a:last_a + 1, :], bcum[last_b:last_b + 1, :])

    qf = q_ref[...].astype(F32) * (GLA_DK ** -0.5)
    kf = k_ref[...].astype(F32)
    qe = qf * jnp.exp(bcum - refb)
    ke = (kf * jnp.exp(refb - bcum)).astype(BF16)
    kd = kf * jnp.exp(lastb - bcum)
    qb = qf * jnp.exp(bcum)

    first_is_a = not backward
    for p in range(GLA_HEADS // 2):
        sl = slice(p * LANES, (p + 1) * LANES)
        qe_p = qe[:, sl]
        att2 = _dot_nt(jnp.concatenate([qe_p * lo_f, qe_p * hi_f], axis=0).astype(BF16), ke[:, sl])
        kd_t = kd[:, sl].T
        qb_p = qb[:, sl]
        s_prev = s_ref[sl, :]
        v0 = v_ref[:, (2 * p) * GLA_DV:(2 * p + 1) * GLA_DV]
        v1 = v_ref[:, (2 * p + 1) * GLA_DV:(2 * p + 2) * GLA_DV]
        v01 = jnp.concatenate([v0, v1], axis=0)
        top = lax.broadcasted_iota(jnp.int32, (LANES, 1), 0) < HALF
        top_f = jnp.where(top, 1.0, 0.0)
        bot_f = 1.0 - top_f

        def chunk_state(cmask):
            xk = kd_t * cmask
            lhs = jnp.concatenate([xk * top_f, xk * bot_f], axis=1).astype(BF16)
            return jnp.dot(lhs, v01, preferred_element_type=F32)

        s_a = chunk_state(col_a)
        s_b = chunk_state(col_b)
        la = jnp.broadcast_to(jnp.exp(bcum[last_a:last_a + 1, sl]), (LANES, LANES)).T
        lb = jnp.broadcast_to(jnp.exp(bcum[last_b:last_b + 1, sl]), (LANES, LANES)).T
        if first_is_a:
            s_mid = s_prev * la + s_a
            s_new = s_mid * lb + s_b
            r_first, r_second = row_a, row_b
        else:
            s_mid = s_prev * lb + s_b
            s_new = s_mid * la + s_a
            r_first, r_second = row_b, row_a
        s_ref[sl, :] = s_new
        rhs_tail = jnp.concatenate([s_prev.astype(BF16), s_mid.astype(BF16)], axis=0)
        for hh in range(2):
            lm = lo_f if hh == 0 else hi_f
            att = jnp.where(keep, att2[hh * t:(hh + 1) * t, :], 0.0)
            qh = qb_p * lm
            lhs = jnp.concatenate([att, qh * r_first, qh * r_second], axis=1).astype(BF16)
            rhs = jnp.concatenate([v0 if hh == 0 else v1, rhs_tail], axis=0)
            o = jnp.dot(lhs, rhs, preferred_element_type=F32)
            h = 2 * p + hh
            o_ref[:, h * GLA_DV:(h + 1) * GLA_DV] = o.astype(o_ref.dtype)


def _gla_kernel(q_f, k_f, v_f, lr_f, q_b, k_b, v_b, lr_b, w2f_ref, w2b_ref, bf_ref, bb_ref,
                of_ref, ob_ref, sf_ref, sb_ref):
    @pl.when(pl.program_id(1) == 0)
    def _():
        sf_ref[...] = jnp.zeros_like(sf_ref)
        sb_ref[...] = jnp.zeros_like(sb_ref)

    _gla_direction(q_f, k_f, v_f, lr_f, w2f_ref, bf_ref, sf_ref, of_ref, backward=False)
    _gla_direction(q_b, k_b, v_b, lr_b, w2b_ref, bb_ref, sb_ref, ob_ref, backward=True)


def _gla(proj, w2f, w2b, bias_f, bias_b):
    b, l, _ = proj.shape
    t = GLA_TILE
    nt = l // t
    fwd = lambda bi, i: i
    bwd = lambda bi, i: nt - 1 - i

    def specs(ci):
        return [
            pl.BlockSpec((None, t, GLA_QK_WIDTH), lambda bi, i: (bi, ci(bi, i), C_GQ // GLA_QK_WIDTH)),
            pl.BlockSpec((None, t, GLA_QK_WIDTH), lambda bi, i: (bi, ci(bi, i), C_GK // GLA_QK_WIDTH)),
            pl.BlockSpec((None, t, GLA_WIDTH), lambda bi, i: (bi, ci(bi, i), C_GV // GLA_WIDTH)),
            pl.BlockSpec((None, t, LANES), lambda bi, i: (bi, ci(bi, i), C_GLR // LANES)),
        ]

    const = lambda r, n: pl.BlockSpec((r, n), lambda bi, i: (0, 0))
    return pl.pallas_call(
        _gla_kernel,
        grid=(b, nt),
        in_specs=specs(fwd) + specs(bwd) + [const(LANES, GLA_QK_WIDTH), const(LANES, GLA_QK_WIDTH),
                                            const(1, GLA_QK_WIDTH), const(1, GLA_QK_WIDTH)],
        out_specs=[pl.BlockSpec((None, t, GLA_WIDTH), lambda bi, i: (bi, i, 0)),
                   pl.BlockSpec((None, t, GLA_WIDTH), lambda bi, i: (bi, nt - 1 - i, 0))],
        out_shape=[jax.ShapeDtypeStruct((b, l, GLA_WIDTH), BF16)] * 2,
        scratch_shapes=[pltpu.VMEM((GLA_QK_WIDTH, GLA_DV), F32)] * 2,
        compiler_params=_params(("parallel", "arbitrary")),
        name="gla_scan",
    )(proj, proj, proj, proj, proj, proj, proj, proj, w2f, w2b, bias_f, bias_b)


def _head_norm_rope(x, ones_blk, w_row, cos, sin_signed):
    width = x.shape[-1]
    ss = _exact_rdot(x * x, ones_blk)
    xn = x * lax.rsqrt(ss * (1.0 / ATT_HEAD_DIM) + EPS) * w_row
    lane = lax.broadcasted_iota(jnp.int32, (1, width), 1)
    first_half = (lane & ROPE_N_FREQ) == 0
    partner = jnp.where(first_half,
                        pltpu.roll(xn, width - ROPE_N_FREQ, 1),
                        pltpu.roll(xn, ROPE_N_FREQ, 1))
    return xn * cos + partner * sin_signed


def _attn_prep_kernel(q_ref, k_ref, cos_ref, sin_ref, qw_ref, kw_ref, oq_ref, ok_ref):
    def ones_blk(width):
        r = lax.shift_right_logical(lax.broadcasted_iota(jnp.int32, (width, width), 0), 6)
        c = lax.shift_right_logical(lax.broadcasted_iota(jnp.int32, (width, width), 1), 6)
        return jnp.where(r == c, 1.0, 0.0).astype(BF16)

    cos = cos_ref[...]
    sin = sin_ref[...]
    reps = ATT_WIDTH // LANES
    q = _head_norm_rope(q_ref[...].astype(F32), ones_blk(ATT_WIDTH), qw_ref[...],
                        jnp.concatenate([cos] * reps, axis=1), jnp.concatenate([sin] * reps, axis=1))
    oq_ref[...] = (q * (ATT_HEAD_DIM ** -0.5)).astype(oq_ref.dtype)
    k = _head_norm_rope(k_ref[...].astype(F32), ones_blk(ATT_KV_WIDTH), kw_ref[...], cos, sin)
    ok_ref[...] = k.T.astype(ok_ref.dtype)


def _attn_prep(proj, cos, sin, qw_row, kw_row, tl):
    b, l, _ = proj.shape
    return pl.pallas_call(
        _attn_prep_kernel,
        grid=(b, l // tl),
        in_specs=[pl.BlockSpec((None, tl, ATT_WIDTH), lambda bi, i: (bi, i, C_AQ // ATT_WIDTH)),
                  pl.BlockSpec((None, tl, ATT_KV_WIDTH), lambda bi, i: (bi, i, C_AK // ATT_KV_WIDTH)),
                  pl.BlockSpec((tl, LANES), lambda bi, i: (i, 0)),
                  pl.BlockSpec((tl, LANES), lambda bi, i: (i, 0)),
                  pl.BlockSpec((1, ATT_WIDTH), lambda bi, i: (0, 0)),
                  pl.BlockSpec((1, ATT_KV_WIDTH), lambda bi, i: (0, 0))],
        out_specs=[pl.BlockSpec((None, tl, ATT_WIDTH), lambda bi, i: (bi, i, 0)),
                   pl.BlockSpec((None, ATT_KV_WIDTH, tl), lambda bi, i: (bi, 0, i))],
        out_shape=[jax.ShapeDtypeStruct((b, l, ATT_WIDTH), BF16),
                   jax.ShapeDtypeStruct((b, ATT_KV_WIDTH, l), BF16)],
        compiler_params=_params(("parallel", "parallel")),
        name="attn_prep",
    )(proj, proj, cos, sin, qw_row, kw_row)


ATT_GROUP = ATT_HEADS // ATT_KV_HEADS
NEG_BIG = -1e30


def _flash_kernel(q_ref, kt_ref, v_ref, o_ref, qs_ref, m_ref, l_ref, acc_ref, *, tq):
    g = pl.program_id(1)
    ki = pl.program_id(3)

    @pl.when(ki == 0)
    def _():
        for h in range(ATT_GROUP):
            qs_ref[h * tq:(h + 1) * tq, :] = q_ref[:, h * ATT_HEAD_DIM:(h + 1) * ATT_HEAD_DIM]
        m_ref[...] = jnp.full_like(m_ref, NEG_BIG)
        l_ref[...] = jnp.zeros_like(l_ref)
        acc_ref[...] = jnp.zeros_like(acc_ref)

    s = jnp.dot(qs_ref[...], kt_ref[...], preferred_element_type=F32)
    m_prev = m_ref[...]
    m_new = jnp.maximum(m_prev, jnp.max(s, axis=-1, keepdims=True))
    alpha = jnp.exp(m_prev - m_new)
    p = jnp.exp(s - m_new[:, 0:1])
    l_ref[...] = alpha * l_ref[...] + jnp.sum(p, axis=-1, keepdims=True)
    acc_ref[...] = alpha * acc_ref[...] + jnp.dot(p.astype(BF16), v_ref[...],
                                                   preferred_element_type=F32)
    m_ref[...] = m_new

    @pl.when(ki == pl.num_programs(3) - 1)
    def _():
        o = acc_ref[...] / l_ref[...]
        o = jnp.where(g == 0, o[:, 0:ATT_HEAD_DIM], o[:, ATT_HEAD_DIM:2 * ATT_HEAD_DIM])
        for h in range(ATT_GROUP):
            o_ref[:, h * ATT_HEAD_DIM:(h + 1) * ATT_HEAD_DIM] = (
                o[h * tq:(h + 1) * tq, :].astype(o_ref.dtype))


def _flash(qr, kt, proj, tq, tk):
    b, l, _ = qr.shape
    gw = ATT_GROUP * ATT_HEAD_DIM
    return pl.pallas_call(
        functools.partial(_flash_kernel, tq=tq),
        grid=(b, ATT_KV_HEADS, l // tq, l // tk),
        in_specs=[pl.BlockSpec((None, tq, gw), lambda bi, g, qi, ki: (bi, qi, g)),
                  pl.BlockSpec((None, ATT_HEAD_DIM, tk), lambda bi, g, qi, ki: (bi, g, ki)),
                  pl.BlockSpec((None, tk, ATT_KV_WIDTH),
                               lambda bi, g, qi, ki: (bi, ki, C_AV // ATT_KV_WIDTH))],
        out_specs=pl.BlockSpec((None, tq, gw), lambda bi, g, qi, ki: (bi, qi, g)),
        out_shape=jax.ShapeDtypeStruct((b, l, ATT_WIDTH), BF16),
        scratch_shapes=[pltpu.VMEM((ATT_GROUP * tq, ATT_HEAD_DIM), BF16),
                        pltpu.VMEM((ATT_GROUP * tq, LANES), F32),
                        pltpu.VMEM((ATT_GROUP * tq, LANES), F32),
                        pltpu.VMEM((ATT_GROUP * tq, LANES), F32)],
        compiler_params=_params(("parallel", "parallel", "parallel", "arbitrary")),
        name="flash_attn",
    )(qr, kt, proj)


def _merge_kernel(gates_ref, z_ref, gog_ref, yf_ref, yb_ref, of_ref, ob_ref, oa_ref, x_ref,
                  snw_ref, gnw_ref, wbs_ref, wbg_ref, wba_ref, wo_ref, o_ref):
    y = (yf_ref[...].astype(F32) + yb_ref[...].astype(F32)) * _silu(z_ref[...].astype(F32))
    ms = jnp.mean(y * y, axis=-1, keepdims=True)
    o_ssd = (y * lax.rsqrt(ms + EPS) * snw_ref[...]).astype(BF16)

    og = of_ref[...].astype(F32) + ob_ref[...].astype(F32)
    parts = []
    for h in range(GLA_HEADS):
        oh = og[:, h * GLA_DV:(h + 1) * GLA_DV]
        mh = jnp.mean(oh * oh, axis=-1, keepdims=True)
        parts.append(oh * lax.rsqrt(mh + EPS) * gnw_ref[...])
    o_gla = (jnp.concatenate(parts, axis=1) * _silu(gog_ref[...].astype(F32))).astype(BF16)

    def gate(i):
        return _sigmoid(gates_ref[:, i * D_MODEL:(i + 1) * D_MODEL].astype(F32))

    merged = gate(0) * jnp.dot(o_ssd, wbs_ref[...], preferred_element_type=F32)
    merged = merged + gate(1) * jnp.dot(o_gla, wbg_ref[...], preferred_element_type=F32)
    merged = merged + gate(2) * jnp.dot(oa_ref[...], wba_ref[...], preferred_element_type=F32)
    o_ref[...] = x_ref[...] + jnp.dot(merged.astype(BF16), wo_ref[...], preferred_element_type=F32)


def _merge(proj2d, yf, yb, of, ob, oa, x2d, snw, gnw, wbs, wbg, wba, wo, tm):
    t = x2d.shape[0]
    rows = lambda w, cb: pl.BlockSpec((tm, w), lambda i: (i, cb))
    const = lambda r, n: pl.BlockSpec((r, n), lambda i: (0, 0))
    return pl.pallas_call(
        _merge_kernel,
        grid=(t // tm,),
        in_specs=[rows(3 * D_MODEL, 0), rows(SSD_WIDTH, C_Z // SSD_WIDTH),
                  rows(GLA_WIDTH, C_GOG // GLA_WIDTH),
                  rows(SSD_WIDTH, 0), rows(SSD_WIDTH, 0), rows(GLA_WIDTH, 0), rows(GLA_WIDTH, 0),
                  rows(ATT_WIDTH, 0), rows(D_MODEL, 0),
                  const(1, SSD_WIDTH), const(1, GLA_DV),
                  const(SSD_WIDTH, D_MODEL), const(GLA_WIDTH, D_MODEL), const(ATT_WIDTH, D_MODEL),
                  const(D_MODEL, D_MODEL)],
        out_specs=rows(D_MODEL, 0),
        out_shape=jax.ShapeDtypeStruct((t, D_MODEL), F32),
        compiler_params=_params(("parallel",)),
        name="merge",
    )(proj2d, proj2d, proj2d, yf, yb, of, ob, oa, x2d, snw, gnw, wbs, wbg, wba, wo)


def _ffn_kernel(x_ref, g_ref, wg_ref, wu_ref, wo_ref, fw_ref, o_ref, h_ref, acc_ref, *, final_norm):
    j = pl.program_id(1)

    @pl.when(j == 0)
    def _():
        x = x_ref[...]
        ms = jnp.mean(x * x, axis=-1, keepdims=True)
        h_ref[...] = (x * lax.rsqrt(ms + EPS) * g_ref[...]).astype(BF16)
        acc_ref[...] = jnp.zeros_like(acc_ref)

    h = h_ref[...]
    gate = jnp.dot(h, wg_ref[...], preferred_element_type=F32)
    up = jnp.dot(h, wu_ref[...], preferred_element_type=F32)
    act = (_silu(gate) * up).astype(BF16)
    acc_ref[...] += jnp.dot(act, wo_ref[...], preferred_element_type=F32)

    @pl.when(j == pl.num_programs(1) - 1)
    def _():
        y = x_ref[...] + acc_ref[...]
        if final_norm:
            ms = jnp.mean(y * y, axis=-1, keepdims=True)
            y = y * lax.rsqrt(ms + EPS) * fw_ref[...]
        o_ref[...] = y


def _ffn(x2d, g, w_in, w_out, fw, tm, tf, final_norm):
    t, d = x2d.shape
    nf = D_FF // tf
    return pl.pallas_call(
        functools.partial(_ffn_kernel, final_norm=final_norm),
        grid=(t // tm, nf),
        in_specs=[pl.BlockSpec((tm, d), lambda i, j: (i, 0)),
                  pl.BlockSpec((1, d), lambda i, j: (0, 0)),
                  pl.BlockSpec((d, tf), lambda i, j: (0, j)),
                  pl.BlockSpec((d, tf), lambda i, j: (0, nf + j)),
                  pl.BlockSpec((tf, d), lambda i, j: (j, 0)),
                  pl.BlockSpec((1, d), lambda i, j: (0, 0))],
        out_specs=pl.BlockSpec((tm, d), lambda i, j: (i, 0)),
        out_shape=jax.ShapeDtypeStruct((t, d), F32),
        scratch_shapes=[pltpu.VMEM((tm, d), BF16), pltpu.VMEM((tm, d), F32)],
        compiler_params=_params(("parallel", "arbitrary")),
        name="ffn",
    )(x2d, g, w_in, w_in, w_out, fw)


def _pad_cols(a, width):
    return jnp.pad(a, ((0, 0), (0, width - a.shape[1])))


def _reorder_w_in(w):
    sizes = (SSD_WIDTH, SSD_CONV_CH, 2 * SSD_HEADS, GLA_QK_WIDTH, GLA_QK_WIDTH, GLA_WIDTH, GLA_WIDTH,
             2 * GLA_RANK, ATT_WIDTH, ATT_KV_WIDTH, ATT_KV_WIDTH, 3 * D_MODEL)
    cuts, acc = [], 0
    for s in sizes[:-1]:
        acc += s
        cuts.append(acc)
    z, xbc, dt, gq, gk, gv, gog, glr, aq, ak, av, gates = jnp.split(w, cuts, axis=1)
    out = jnp.concatenate([gates, z, xbc, gv, gog, aq, gq, gk, ak, av,
                           _pad_cols(dt, LANES), _pad_cols(glr, LANES)], axis=1)
    return out.astype(BF16)


def _rope_tables(length):
    rows = length // GRID_W
    row = jnp.broadcast_to(jnp.arange(rows, dtype=F32)[:, None], (rows, GRID_W)).reshape(length)
    col = jnp.broadcast_to(jnp.arange(GRID_W, dtype=F32)[None, :], (rows, GRID_W)).reshape(length)
    inv = jnp.float32(ROPE_THETA) ** (-jnp.arange(ROPE_N_FREQ, dtype=F32) / ROPE_N_FREQ)
    ar = row[:, None] * inv
    ac = col[:, None] * inv
    cos = jnp.concatenate([jnp.cos(ar), jnp.cos(ar), jnp.cos(ac), jnp.cos(ac)], axis=1)
    sin = jnp.concatenate([-jnp.sin(ar), jnp.sin(ar), -jnp.sin(ac), jnp.sin(ac)], axis=1)
    reps = LANES // ATT_HEAD_DIM
    return jnp.tile(cos, (1, reps)), jnp.tile(sin, (1, reps))


def _pick(n, pref):
    for c in pref:
        if n % c == 0:
            return c
    return n


def _layer(x, p, final_w, final_norm):
    b, l, d = x.shape
    t = b * l
    x2d = x.reshape(t, d)
    proj2d = _in_proj(x2d, p["norm1"], p["w_in"], _pick(t, (2048, 1024, 512, 256)), 512)
    proj = proj2d.reshape(b, l, PROJ_W)

    xbc = _conv(proj, p["conv_w"], p["conv_b"], _pick(l, (512, 256, 128)))
    yf, yb = _ssd(xbc, proj, p["dt_bias"], p["a_log"], p["d_skip"])
    of, ob = _gla(proj, p["gla_w2f"], p["gla_w2b"], p["gla_bf"], p["gla_bb"])

    cos, sin = _rope_tables(l)
    qr, kt = _attn_prep(proj, cos, sin, p["qw"], p["kw"], _pick(l, (512, 256, 128)))
    oa = _flash(qr, kt, proj, _pick(l, (256, 128)), _pick(l, (1024, 512, 256, 128)))

    x2d = _merge(proj2d, yf.reshape(t, -1), yb.reshape(t, -1), of.reshape(t, -1), ob.reshape(t, -1),
                 oa.reshape(t, -1), x2d, p["ssd_norm"], p["gla_norm"], p["w_br_ssd"], p["w_br_gla"],
                 p["w_br_attn"], p["w_out"], _pick(t, (512, 256, 128)))
    x2d = _ffn(x2d, p["norm2"], p["w_ffn_in"], p["w_ffn_out"], final_w,
               _pick(t, (1024, 512, 256, 128)), 256, final_norm)
    return x2d.reshape(b, l, d)


def _layer_params(l, w_in, conv_w, conv_b, ssd_a_log_f, ssd_a_log_b, ssd_dt_bias_f, ssd_dt_bias_b,
                  ssd_d, ssd_norm_w, gla_w2_f, gla_b_f, gla_w2_b, gla_b_b, gla_norm_w, att_q_norm_w,
                  att_k_norm_w, w_br_ssd, w_br_gla, w_br_attn, w_out, norm1_w, norm2_w, w_ffn_in,
                  w_ffn_out):
    row = lambda v: v.reshape(1, -1).astype(F32)
    lanes = lambda a, b_: _pad_cols(jnp.concatenate([a, b_]).reshape(1, -1).astype(F32), LANES)
    w2 = lambda w, off: jnp.zeros((LANES, GLA_QK_WIDTH), F32).at[off:off + GLA_RANK].set(w).astype(BF16)
    return dict(
        norm1=row(norm1_w[l]), w_in=_reorder_w_in(w_in[l]),
        conv_w=jnp.pad(conv_w[l].astype(F32), ((0, 8 - SSD_CONV), (0, 0))), conv_b=row(conv_b[l]),
        dt_bias=lanes(ssd_dt_bias_f[l], ssd_dt_bias_b[l]),
        a_log=lanes(ssd_a_log_f[l], ssd_a_log_b[l]),
        d_skip=row(jnp.repeat(ssd_d[l], SSD_HEAD_DIM)),
        gla_w2f=w2(gla_w2_f[l], 0), gla_w2b=w2(gla_w2_b[l], GLA_RANK),
        gla_bf=row(gla_b_f[l]), gla_bb=row(gla_b_b[l]),
        qw=row(jnp.tile(att_q_norm_w[l], ATT_HEADS)), kw=row(jnp.tile(att_k_norm_w[l], ATT_KV_HEADS)),
        ssd_norm=row(ssd_norm_w[l]), gla_norm=row(gla_norm_w[l]),
        w_br_ssd=w_br_ssd[l].astype(BF16), w_br_gla=w_br_gla[l].astype(BF16),
        w_br_attn=w_br_attn[l].astype(BF16), w_out=w_out[l].astype(BF16),
        norm2=row(norm2_w[l]), w_ffn_in=w_ffn_in[l].astype(BF16), w_ffn_out=w_ffn_out[l].astype(BF16),
    )


def kernel(x_prompt, x_sample, w_in, conv_w, conv_b, ssd_a_log_f, ssd_a_log_b, ssd_dt_bias_f,
           ssd_dt_bias_b, ssd_d, ssd_norm_w, gla_w2_f, gla_b_f, gla_w2_b, gla_b_b, gla_norm_w,
           att_q_norm_w, att_k_norm_w, w_br_ssd, w_br_gla, w_br_attn, w_out, norm1_w, norm2_w,
           w_ffn_in, w_ffn_out, final_norm_w):
    depth = w_in.shape[0]
    stacked = (w_in, conv_w, conv_b, ssd_a_log_f, ssd_a_log_b, ssd_dt_bias_f, ssd_dt_bias_b, ssd_d,
               ssd_norm_w, gla_w2_f, gla_b_f, gla_w2_b, gla_b_b, gla_norm_w, att_q_norm_w,
               att_k_norm_w, w_br_ssd, w_br_gla, w_br_attn, w_out, norm1_w, norm2_w, w_ffn_in,
               w_ffn_out)
    layers = [_layer_params(l, *stacked) for l in range(depth)]
    final_w = final_norm_w.reshape(1, -1).astype(F32)

    def trunk(x):
        for l in range(depth):
            x = _layer(x, layers[l], final_w, l == depth - 1)
        return x

    return (trunk(x_prompt), trunk(x_sample))
```
